```python
import jax, jax.numpy as jnp
from jax import lax
import numpy as np

D_MODEL = 2048
BATCH = 4
SEQ = 2048
DEPTH = 1
DEC_BATCH = 128
DEC_SEQ = 8
PAST_LEN = 16384
PAGE_SIZE = 128

N_META = 16
D_A = 1536
LRU_BLOCK = 128
N_LRU_BLOCKS = D_A // LRU_BLOCK
CONV_A = 4
LRU_C = 8.0
D_B = 1024
N_SC_GROUPS = 8
CONV_B = 3
D_MIX = D_A + D_B
D_IN = 2 * D_A + 3 * D_B
D_FF = 3 * D_MODEL
CONV_F = 3
EPS = 1e-6

kernel_name = "hymba_rglru_shortconv_convffn_step"


def rmsnorm(x, g):
    xf = x.astype(jnp.float32)
    ms = jnp.mean(xf * xf, axis=-1, keepdims=True)
    return (xf * lax.rsqrt(ms + EPS) * g.astype(jnp.float32)).astype(x.dtype)


def group_rmsnorm(y, g, n_groups):
    shp = y.shape
    yf = y.astype(jnp.float32).reshape(shp[:-1] + (n_groups, shp[-1] // n_groups))
    ms = jnp.mean(yf * yf, axis=-1, keepdims=True)
    yn = (yf * lax.rsqrt(ms + EPS)).reshape(shp)
    return (yn * g.astype(jnp.float32)).astype(y.dtype)


def causal_dwconv(x, buf, w):
    width = w.shape[0]
    t_len = x.shape[1]
    xp = jnp.concatenate([buf.astype(x.dtype), x], axis=1)
    w = w.astype(x.dtype)
    y = xp[:, 0:t_len] * w[0]
    for k in range(1, width):
        y = y + xp[:, k:k + t_len] * w[k]
    return y, xp[:, t_len:]


def rg_lru(x, h0, w_gate_a, b_gate_a, w_gate_x, b_gate_x, lam):
    bsz, t_len, _ = x.shape
    xf = x.astype(jnp.float32)
    xb = xf.reshape(bsz, t_len, N_LRU_BLOCKS, LRU_BLOCK)
    r = jax.nn.sigmoid(jnp.einsum("btnk,nkj->btnj", xb, w_gate_a.astype(jnp.float32)).reshape(bsz, t_len, D_A)
                       + b_gate_a.astype(jnp.float32))
    i = jax.nn.sigmoid(jnp.einsum("btnk,nkj->btnj", xb, w_gate_x.astype(jnp.float32)).reshape(bsz, t_len, D_A)
                       + b_gate_x.astype(jnp.float32))
    log_a = -LRU_C * r * jax.nn.softplus(-lam.astype(jnp.float32))
    a = jnp.exp(log_a)
    u = jnp.sqrt(-jnp.expm1(2.0 * log_a)) * (i * xf)

    def step(h, au):
        a_t, u_t = au
        h = a_t * h + u_t
        return h, h

    h_last, hs = lax.scan(step, h0.astype(jnp.float32),
                          (jnp.swapaxes(a, 0, 1), jnp.swapaxes(u, 0, 1)))
    return jnp.swapaxes(hs, 0, 1), h_last


def hybrid_layer(x, st_h, st_rconv, st_sconv, st_fconv,
                 g_mix, w_in, conv_a_w, conv_a_b, w_gate_a, b_gate_a, w_gate_x, b_gate_x,
                 lru_lambda, conv_b_w, g_out_a, g_out_b, w_o, g_ffn, w_up, conv_f_w,
                 conv_f_b, w_down):
    xn = rmsnorm(x, g_mix)
    z = jnp.einsum("btd,de->bte", xn, w_in)
    xa, ga, gb, gc, vb = jnp.split(z, [D_A, 2 * D_A, 2 * D_A + D_B, 2 * D_A + 2 * D_B], axis=-1)
    xa_c, new_rconv = causal_dwconv(xa, st_rconv, conv_a_w)
    xa_c = xa_c + conv_a_b
    hs, new_h = rg_lru(xa_c, st_h, w_gate_a, b_gate_a, w_gate_x, b_gate_x, lru_lambda)
    y_a = jax.nn.gelu(ga) * hs.astype(x.dtype)
    u = gc * vb
    uc, new_sconv = causal_dwconv(u, st_sconv, conv_b_w)
    y_b = gb * uc
    y_mix = jnp.concatenate([group_rmsnorm(y_a, g_out_a, N_LRU_BLOCKS),
                             group_rmsnorm(y_b, g_out_b, N_SC_GROUPS)], axis=-1)
    x = x + jnp.einsum("bte,ed->btd", y_mix, w_o)
    xn2 = rmsnorm(x, g_ffn)
    up = jnp.einsum("btd,df->btf", xn2, w_up)
    gate, val = jnp.split(up, [D_FF], axis=-1)
    gate_c, new_fconv = causal_dwconv(gate, st_fconv, conv_f_w)
    hid = jax.nn.gelu(gate_c + conv_f_b) * val
    x = x + jnp.einsum("btf,fd->btd", hid, w_down)
    return x, new_h, new_rconv, new_sconv, new_fconv


def setup_inputs(seed: int = 0) -> dict:
    key = jax.random.key(seed)
    ks = jax.random.split(key, 32)
    f32 = jnp.float32
    L = DEPTH

    def nrm(k, shape, scale):
        return jax.random.normal(k, shape, f32) * scale

    a0 = jax.random.uniform(ks[15], (L, D_A), f32, 0.9, 0.999)
    return {
        "x_prompt": nrm(ks[0], (BATCH, SEQ, D_MODEL), 1.0),
        "x_sample": nrm(ks[1], (DEC_BATCH, DEC_SEQ, D_MODEL), 1.0),
        "state_lru_h": nrm(ks[2], (L, DEC_BATCH, D_A), 0.5),
        "state_lru_conv": nrm(ks[3], (L, DEC_BATCH, CONV_A - 1, D_A), 1.0),
        "state_sconv": nrm(ks[4], (L, DEC_BATCH, CONV_B - 1, D_B), 1.0),
        "state_ffn_conv": nrm(ks[5], (L, DEC_BATCH, CONV_F - 1, D_FF), 1.0),
        "meta_tokens": nrm(ks[6], (N_META, D_MODEL), 1.0),
        "g_mix": 1.0 + nrm(ks[7], (L, D_MODEL), 0.02),
        "w_in": nrm(ks[8], (L, D_MODEL, D_IN), D_MODEL ** -0.5),
        "conv_a_w": nrm(ks[9], (L, CONV_A, D_A), CONV_A ** -0.5),
        "conv_a_b": nrm(ks[10], (L, D_A), 0.02),
        "w_gate_a": nrm(ks[11], (L, N_LRU_BLOCKS, LRU_BLOCK, LRU_BLOCK), LRU_BLOCK ** -0.5),
        "b_gate_a": nrm(ks[12], (L, D_A), 0.02),
        "w_gate_x": nrm(ks[13], (L, N_LRU_BLOCKS, LRU_BLOCK, LRU_BLOCK), LRU_BLOCK ** -0.5),
        "b_gate_x": nrm(ks[14], (L, D_A), 0.02),
        "lru_lambda": jnp.log(a0) - jnp.log1p(-a0),
        "conv_b_w": nrm(ks[16], (L, CONV_B, D_B), CONV_B ** -0.5),
        "g_out_a": 1.0 + nrm(ks[17], (L, D_A), 0.02),
        "g_out_b": 1.0 + nrm(ks[18], (L, D_B), 0.02),
        "w_o": nrm(ks[19], (L, D_MIX, D_MODEL), D_MIX ** -0.5),
        "g_ffn": 1.0 + nrm(ks[20], (L, D_MODEL), 0.02),
        "w_up": nrm(ks[21], (L, D_MODEL, 2 * D_FF), D_MODEL ** -0.5),
        "conv_f_w": nrm(ks[22], (L, CONV_F, D_FF), CONV_F ** -0.5),
        "conv_f_b": nrm(ks[23], (L, D_FF), 0.02),
        "w_down": nrm(ks[24], (L, D_FF, D_MODEL), D_FF ** -0.5),
        "g_final": 1.0 + nrm(ks[25], (D_MODEL,), 0.02),
    }


def reference(x_prompt, x_sample, state_lru_h, state_lru_conv, state_sconv, state_ffn_conv,
              meta_tokens, g_mix, w_in, conv_a_w, conv_a_b, w_gate_a, b_gate_a, w_gate_x,
              b_gate_x, lru_lambda, conv_b_w, g_out_a, g_out_b, w_o, g_ffn, w_up, conv_f_w,
              conv_f_b, w_down, g_final):
    dt = x_prompt.dtype
    meta = jnp.broadcast_to(meta_tokens.astype(dt)[None], (BATCH, N_META, D_MODEL))
    xp = jnp.concatenate([meta, x_prompt], axis=1)
    xs = x_sample
    zero_h = jnp.zeros((BATCH, D_A), jnp.float32)
    zero_rconv = jnp.zeros((BATCH, CONV_A - 1, D_A), dt)
    zero_sconv = jnp.zeros((BATCH, CONV_B - 1, D_B), dt)
    zero_fconv = jnp.zeros((BATCH, CONV_F - 1, D_FF), dt)

    p_h, p_rc, p_sc, p_fc = [], [], [], []
    s_h, s_rc, s_sc, s_fc = [], [], [], []
    for l in range(DEPTH):
        lw = (g_mix[l], w_in[l], conv_a_w[l], conv_a_b[l], w_gate_a[l], b_gate_a[l],
              w_gate_x[l], b_gate_x[l], lru_lambda[l], conv_b_w[l], g_out_a[l], g_out_b[l],
              w_o[l], g_ffn[l], w_up[l], conv_f_w[l], conv_f_b[l], w_down[l])
        xp, h1, rc1, sc1, fc1 = hybrid_layer(xp, zero_h, zero_rconv, zero_sconv, zero_fconv, *lw)
        xs, h2, rc2, sc2, fc2 = hybrid_layer(xs, state_lru_h[l], state_lru_conv[l],
                                             state_sconv[l], state_ffn_conv[l], *lw)
        p_h.append(h1); p_rc.append(rc1); p_sc.append(sc1); p_fc.append(fc1)
        s_h.append(h2); s_rc.append(rc2); s_sc.append(sc2); s_fc.append(fc2)

    y_prompt = rmsnorm(xp, g_final)[:, N_META:]
    y_sample = rmsnorm(xs, g_final)
    return (y_prompt, y_sample,
            jnp.stack(p_h), jnp.stack(p_rc), jnp.stack(p_sc), jnp.stack(p_fc),
            jnp.stack(s_h), jnp.stack(s_rc), jnp.stack(s_sc), jnp.stack(s_fc))
```

```python
import functools

import jax
import jax.numpy as jnp
from jax import lax
from jax.experimental import pallas as pl
from jax.experimental.pallas import tpu as pltpu

D_MODEL = 2048
D_A = 1536
D_B = 1024
HEAD = 128
D_MIX = D_A + D_B
D_FF = 3 * D_MODEL
CONV_A = 4
CONV_B = 3
CONV_F = 3
LRU_C = 8.0
EPS = 1e-6

SUBLANES = 8
VMEM_LIMIT_BYTES = 58 * 1024 * 1024

F32 = jnp.float32
BF16 = jnp.bfloat16


def _shift_rows(x, hist, s, seqs):
    rb, c = x.shape
    if s == 0:
        return x
    if seqs:
        x3 = x.reshape(rb // SUBLANES, SUBLANES, c)
        h3 = hist.reshape(rb // SUBLANES, SUBLANES, c)
        t = lax.broadcasted_iota(jnp.int32, (1, SUBLANES, 1), 1)
        y = jnp.where(t < s, pltpu.roll(h3, s, 1), pltpu.roll(x3, s, 1))
        return y.reshape(rb, c)
    xx = jnp.concatenate([hist, x], axis=0)
    return xx[SUBLANES - s:SUBLANES - s + rb]


def _causal_conv(x, hist, w, seqs):
    width = w.shape[0]
    y = _shift_rows(x, hist, width - 1, seqs) * w[0:1]
    for k in range(1, width):
        y = y + _shift_rows(x, hist, width - 1 - k, seqs) * w[k:k + 1]
    return y


def _scan8(a, u):
    t = lax.broadcasted_iota(jnp.int32, (1, SUBLANES, 1), 1)
    for s in (1, 2, 4):
        a_s = pltpu.roll(a, s, 1)
        u_s = pltpu.roll(u, s, 1)
        m = t >= s
        u = jnp.where(m, u + a * u_s, u)
        a = jnp.where(m, a * a_s, a)
    return a, u


def _head_rmsnorm(y, g):
    c = y.shape[1]
    parts = []
    for h in range(c // HEAD):
        yh = y[:, h * HEAD:(h + 1) * HEAD]
        ms = jnp.mean(yh * yh, axis=-1, keepdims=True)
        parts.append(yh * lax.rsqrt(ms + EPS))
    yn = parts[0] if len(parts) == 1 else jnp.concatenate(parts, axis=1)
    return yn * g


def _rmsnorm_rows(x, g):
    ms = jnp.mean(x * x, axis=-1, keepdims=True)
    return x * lax.rsqrt(ms + EPS) * g


def _softplus(x):
    return jnp.maximum(x, 0.0) + jnp.log1p(jnp.exp(-jnp.abs(x)))


def _lane_ds(start, size):
    return pl.ds(pl.multiple_of(start, HEAD), size)


def _mixer_kernel(*refs, seqs, tm, n_a, n_b, n_n, ca, cb, nc, rb, cw):
    if seqs:
        (x_ref, h0_ref, hxa_ref, hu_ref, gmix_ref, wa_ref, wg_ref, pa_ref, wb_ref, pb_ref, wo_ref,
         x1_ref, hout_ref, xaout_ref, uout_ref,
         xn_ref, z_ref, xac_ref, gates_ref, ymix_ref) = refs
        cxa_ref = cu_ref = ch_ref = None
    else:
        (x_ref, h0_ref, hxa_ref, hu_ref, gmix_ref, wa_ref, wg_ref, pa_ref, wb_ref, pb_ref, wo_ref,
         x1_ref, hout_ref, xaout_ref, uout_ref,
         xn_ref, z_ref, xac_ref, gates_ref, ymix_ref, cxa_ref, cu_ref, ch_ref) = refs
    t_idx = pl.program_id(1)
    s_idx = pl.program_id(2)
    n_rb = tm // rb
    g_rb = rb // SUBLANES

    def rows(i):
        return pl.ds(pl.multiple_of(i * rb, rb), rb)

    @pl.when(s_idx == 0)
    def _():
        g = gmix_ref[...]

        def body(i, carry):
            xn_ref[rows(i), :] = _rmsnorm_rows(x_ref[rows(i), :], g).astype(BF16)
            return carry

        lax.fori_loop(0, n_rb, body, 0)

    @pl.when(s_idx < n_a)
    def _():
        j = s_idx
        z_ref[:, :2 * ca] = jnp.dot(xn_ref[...], wa_ref[...], preferred_element_type=F32)

        if not seqs:
            @pl.when(t_idx == 0)
            def _():
                cxa_ref[j] = hxa_ref[...]
                ch_ref[j] = h0_ref[...]

        for c0 in range(0, ca, cw):
            w = pa_ref[0:CONV_A, c0:c0 + cw]
            b = pa_ref[CONV_A:CONV_A + 1, c0:c0 + cw]

            def conv_body(i, hist, c0=c0, w=w, b=b):
                xa = z_ref[rows(i), c0:c0 + cw]
                if seqs:
                    hist_i = hxa_ref[rows(i), c0:c0 + cw]
                else:
                    hist_i = hist
                xac_ref[rows(i), c0:c0 + cw] = _causal_conv(xa, hist_i, w, seqs) + b
                if seqs:
                    g0 = pl.multiple_of(i * g_rb, g_rb)
                    xaout_ref[pl.ds(g0, g_rb), :, c0:c0 + cw] = (
                        xa.reshape(g_rb, SUBLANES, cw)[:, SUBLANES - (CONV_A - 1):, :])
                    return hist
                return xa[rb - SUBLANES:]

            if seqs:
                lax.fori_loop(0, n_rb, conv_body, 0)
            else:
                hist = lax.fori_loop(0, n_rb, conv_body, cxa_ref[j, :, c0:c0 + cw])
                cxa_ref[j, :, c0:c0 + cw] = hist
                xaout_ref[:, c0:c0 + cw] = hist[SUBLANES - (CONV_A - 1):]

        for h in range(ca // HEAD):
            gates_ref[:, 2 * HEAD * h:2 * HEAD * (h + 1)] = jnp.dot(
                xac_ref[:, h * HEAD:(h + 1) * HEAD].astype(BF16), wg_ref[h],
                preferred_element_type=F32)

        for h in range(ca // HEAD):
            c0 = h * HEAD
            b_a = pa_ref[5:6, c0:c0 + HEAD]
            b_x = pa_ref[6:7, c0:c0 + HEAD]
            neg_c_sp = -LRU_C * _softplus(-pa_ref[7:8, c0:c0 + HEAD])
            g_out = pa_ref[8:9, c0:c0 + HEAD]

            def lru_body(i, carry, h=h, c0=c0, b_a=b_a, b_x=b_x, neg_c_sp=neg_c_sp, g_out=g_out):
                xc = xac_ref[rows(i), c0:c0 + HEAD]
                r = jax.nn.sigmoid(gates_ref[rows(i), 2 * c0:2 * c0 + HEAD] + b_a)
                gi = jax.nn.sigmoid(gates_ref[rows(i), 2 * c0 + HEAD:2 * c0 + 2 * HEAD] + b_x)
                log_a = r * neg_c_sp
                a = jnp.exp(log_a)
                u = jnp.sqrt(-jnp.tanh(log_a) * (a * a + 1.0)) * (gi * xc)
                a3, u3 = _scan8(a.reshape(g_rb, SUBLANES, HEAD), u.reshape(g_rb, SUBLANES, HEAD))
                if seqs:
                    g0 = pl.multiple_of(i * g_rb, g_rb)
                    h0 = h0_ref[pl.ds(g0, g_rb), c0:c0 + HEAD]
                    h3 = u3 + a3 * h0[:, None, :]
                    hout_ref[pl.ds(g0, g_rb), c0:c0 + HEAD] = h3[:, SUBLANES - 1, :]
                    hs = h3.reshape(rb, HEAD)
                else:
                    parts = []
                    for g in range(g_rb):
                        hg = u3[g] + a3[g] * carry
                        carry = jnp.broadcast_to(hg[SUBLANES - 1:SUBLANES], (SUBLANES, HEAD))
                        parts.append(hg)
                    hs = jnp.concatenate(parts, axis=0)
                ga = z_ref[rows(i), ca + c0:ca + c0 + HEAD]
                y = _head_rmsnorm(jax.nn.gelu(ga) * hs, g_out)
                ymix_ref[rows(i), _lane_ds(j * ca + c0, HEAD)] = y.astype(BF16)
                return carry

            if seqs:
                lax.fori_loop(0, n_rb, lru_body, 0)
            else:
                carry = lax.fori_loop(0, n_rb, lru_body, ch_ref[j, :, c0:c0 + HEAD])
                ch_ref[j, :, c0:c0 + HEAD] = carry
                hout_ref[:, c0:c0 + HEAD] = carry[0:1]

    @pl.when((s_idx >= n_a) & (s_idx < n_a + n_b))
    def _():
        j = s_idx - n_a
        z_ref[:, :3 * cb] = jnp.dot(xn_ref[...], wb_ref[...], preferred_element_type=F32)

        if not seqs:
            @pl.when(t_idx == 0)
            def _():
                cu_ref[j] = hu_ref[...]

        for c0 in range(0, cb, cw):
            w = pb_ref[0:CONV_B, c0:c0 + cw]
            g_out = pb_ref[CONV_B:CONV_B + 1, c0:c0 + cw]

            def sc_body(i, hist, c0=c0, w=w, g_out=g_out):
                gb = z_ref[rows(i), c0:c0 + cw]
                u = z_ref[rows(i), cb + c0:cb + c0 + cw] * z_ref[rows(i), 2 * cb + c0:2 * cb + c0 + cw]
                if seqs:
                    hist_i = hu_ref[rows(i), c0:c0 + cw]
                else:
                    hist_i = hist
                y = _head_rmsnorm(gb * _causal_conv(u, hist_i, w, seqs), g_out)
                ymix_ref[rows(i), _lane_ds(D_A + j * cb + c0, cw)] = y.astype(BF16)
                if seqs:
                    g0 = pl.multiple_of(i * g_rb, g_rb)
                    uout_ref[pl.ds(g0, g_rb), :, c0:c0 + cw] = (
                        u.reshape(g_rb, SUBLANES, cw)[:, SUBLANES - (CONV_B - 1):, :])
                    return hist
                return u[rb - SUBLANES:]

            if seqs:
                lax.fori_loop(0, n_rb, sc_body, 0)
            else:
                hist = lax.fori_loop(0, n_rb, sc_body, cu_ref[j, :, c0:c0 + cw])
                cu_ref[j, :, c0:c0 + cw] = hist
                uout_ref[:, c0:c0 + cw] = hist[SUBLANES - (CONV_B - 1):]

    @pl.when(s_idx >= n_a + n_b)
    def _():
        n = s_idx - (n_a + n_b)
        cols = _lane_ds(n * nc, nc)
        x1_ref[:, cols] = x_ref[:, cols] + jnp.dot(ymix_ref[...], wo_ref[...],
                                                   preferred_element_type=F32)


def _mixer_call(x, h0, hxa, hu, gmix, wa, wg, pa, wb, pb, wo, *, seqs, tm, rb, cw=256):
    bsz, t_len, _ = x.shape
    n_a, _, ca2 = wa.shape
    ca = ca2 // 2
    n_b, _, cb3 = wb.shape
    cb = cb3 // 3
    nc = 512
    n_n = D_MODEL // nc
    n_t = t_len // tm
    n_s = n_a + n_b + n_n
    rb = min(rb, tm)

    def ja(s):
        return jnp.minimum(s, n_a - 1)

    def jb(s):
        return jnp.clip(s - n_a, 0, n_b - 1)

    def jn(s):
        return jnp.clip(s - n_a - n_b, 0, n_n - 1)

    if seqs:
        state_specs = [
            pl.BlockSpec((tm // SUBLANES, ca), lambda b, t, s: (t, ja(s))),
            pl.BlockSpec((tm, ca), lambda b, t, s: (t, ja(s))),
            pl.BlockSpec((tm, cb), lambda b, t, s: (t, jb(s))),
        ]
        n_g = t_len // SUBLANES
        out_shape = [
            jax.ShapeDtypeStruct((bsz, t_len, D_MODEL), F32),
            jax.ShapeDtypeStruct((n_g, D_A), F32),
            jax.ShapeDtypeStruct((n_g, CONV_A - 1, D_A), F32),
            jax.ShapeDtypeStruct((n_g, CONV_B - 1, D_B), F32),
        ]
        out_specs = [
            pl.BlockSpec((None, tm, D_MODEL), lambda b, t, s: (b, t, 0)),
            pl.BlockSpec((tm // SUBLANES, ca), lambda b, t, s: (t, ja(s))),
            pl.BlockSpec((tm // SUBLANES, CONV_A - 1, ca), lambda b, t, s: (t, 0, ja(s))),
            pl.BlockSpec((tm // SUBLANES, CONV_B - 1, cb), lambda b, t, s: (t, 0, jb(s))),
        ]
        carry_scratch = []
    else:
        state_specs = [
            pl.BlockSpec((SUBLANES, ca), lambda b, t, s: (0, ja(s))),
            pl.BlockSpec((SUBLANES, ca), lambda b, t, s: (0, ja(s))),
            pl.BlockSpec((SUBLANES, cb), lambda b, t, s: (0, jb(s))),
        ]
        out_shape = [
            jax.ShapeDtypeStruct((bsz, t_len, D_MODEL), F32),
            jax.ShapeDtypeStruct((bsz, n_t, 1, D_A), F32),
            jax.ShapeDtypeStruct((bsz, n_t, CONV_A - 1, D_A), F32),
            jax.ShapeDtypeStruct((bsz, n_t, CONV_B - 1, D_B), F32),
        ]
        out_specs = [
            pl.BlockSpec((None, tm, D_MODEL), lambda b, t, s: (b, t, 0)),
            pl.BlockSpec((None, None, 1, ca), lambda b, t, s: (b, t, 0, ja(s))),
            pl.BlockSpec((None, None, CONV_A - 1, ca), lambda b, t, s: (b, t, 0, ja(s))),
            pl.BlockSpec((None, None, CONV_B - 1, cb), lambda b, t, s: (b, t, 0, jb(s))),
        ]
        carry_scratch = [
            pltpu.VMEM((n_a, SUBLANES, ca), F32),
            pltpu.VMEM((n_b, SUBLANES, cb), F32),
            pltpu.VMEM((n_a, SUBLANES, ca), F32),
        ]

    in_specs = [pl.BlockSpec((None, tm, D_MODEL), lambda b, t, s: (b, t, 0))] + state_specs + [
        pl.BlockSpec((1, D_MODEL), lambda b, t, s: (0, 0)),
        pl.BlockSpec((None, D_MODEL, 2 * ca), lambda b, t, s: (ja(s), 0, 0)),
        pl.BlockSpec((ca // HEAD, HEAD, 2 * HEAD), lambda b, t, s: (ja(s), 0, 0)),
        pl.BlockSpec((16, ca), lambda b, t, s: (0, ja(s))),
        pl.BlockSpec((None, D_MODEL, 3 * cb), lambda b, t, s: (jb(s), 0, 0)),
        pl.BlockSpec((8, cb), lambda b, t, s: (0, jb(s))),
        pl.BlockSpec((D_MIX, nc), lambda b, t, s: (0, jn(s))),
    ]
    scratch = [
        pltpu.VMEM((tm, D_MODEL), BF16),
        pltpu.VMEM((tm, max(2 * ca, 3 * cb)), F32),
        pltpu.VMEM((tm, ca), F32),
        pltpu.VMEM((tm, 2 * ca), F32),
        pltpu.VMEM((tm, D_MIX), BF16),
    ] + carry_scratch
    kern = functools.partial(_mixer_kernel, seqs=seqs, tm=tm, n_a=n_a, n_b=n_b, n_n=n_n,
                             ca=ca, cb=cb, nc=nc, rb=rb, cw=cw)
    return pl.pallas_call(
        kern,
        grid=(bsz, n_t, n_s),
        in_specs=in_specs,
        out_specs=out_specs,
        out_shape=out_shape,
        scratch_shapes=scratch,
        compiler_params=pltpu.CompilerParams(
            dimension_semantics=("arbitrary", "arbitrary", "arbitrary"),
            vmem_limit_bytes=VMEM_LIMIT_BYTES),
        name="mixer_seqs" if seqs else "mixer_chain",
    )(x, h0, hxa, hu, gmix, wa, wg, pa, wb, pb, wo)


def _ffn_kernel(*refs, seqs, tm, n_f, n_n, fc, nc, rb, cw):
    if seqs:
        (x_ref, hg_ref, gffn_ref, wup_ref, pf_ref, wdn_ref, gfin_ref,
         y_ref, gout_ref,
         xn_ref, up_ref, hid_ref, xo_ref) = refs
        cg_ref = None
    else:
        (x_ref, hg_ref, gffn_ref, wup_ref, pf_ref, wdn_ref, gfin_ref,
         y_ref, gout_ref,
         xn_ref, up_ref, hid_ref, xo_ref, cg_ref) = refs
    t_idx = pl.program_id(1)
    s_idx = pl.program_id(2)
    n_rb = tm // rb
    g_rb = rb // SUBLANES

    def rows(i):
        return pl.ds(pl.multiple_of(i * rb, rb), rb)

    @pl.when(s_idx == 0)
    def _():
        g = gffn_ref[...]

        def body(i, carry):
            xn_ref[rows(i), :] = _rmsnorm_rows(x_ref[rows(i), :], g).astype(BF16)
            return carry

        lax.fori_loop(0, n_rb, body, 0)

    @pl.when(s_idx < n_f)
    def _():
        j = s_idx
        up_ref[...] = jnp.dot(xn_ref[...], wup_ref[...], preferred_element_type=F32)

        if not seqs:
            @pl.when(t_idx == 0)
            def _():
                cg_ref[j] = hg_ref[...]

        for c0 in range(0, fc, cw):
            w = pf_ref[0:CONV_F, c0:c0 + cw]
            b = pf_ref[CONV_F:CONV_F + 1, c0:c0 + cw]

            def ffn_body(i, hist, c0=c0, w=w, b=b):
                gate = up_ref[rows(i), c0:c0 + cw]
                if seqs:
                    hist_i = hg_ref[rows(i), c0:c0 + cw]
                else:
                    hist_i = hist
                hid = jax.nn.gelu(_causal_conv(gate, hist_i, w, seqs) + b) * up_ref[rows(i), fc + c0:fc + c0 + cw]
                hid_ref[rows(i), _lane_ds(j * fc + c0, cw)] = hid.astype(BF16)
                if seqs:
                    g0 = pl.multiple_of(i * g_rb, g_rb)
                    gout_ref[pl.ds(g0, g_rb), :, c0:c0 + cw] = (
                        gate.reshape(g_rb, SUBLANES, cw)[:, SUBLANES - (CONV_F - 1):, :])
                    return hist
                return gate[rb - SUBLANES:]

            if seqs:
                lax.fori_loop(0, n_rb, ffn_body, 0)
            else:
                hist = lax.fori_loop(0, n_rb, ffn_body, cg_ref[j, :, c0:c0 + cw])
                cg_ref[j, :, c0:c0 + cw] = hist
                gout_ref[:, c0:c0 + cw] = hist[SUBLANES - (CONV_F - 1):]

    @pl.when(s_idx >= n_f)
    def _():
        n = s_idx - n_f
        cols = _lane_ds(n * nc, nc)
        xo_ref[:, cols] = x_ref[:, cols] + jnp.dot(hid_ref[...], wdn_ref[...],
                                                   preferred_element_type=F32)

    @pl.when(s_idx == n_f + n_n - 1)
    def _():
        g = gfin_ref[...]

        def body(i, carry):
            y_ref[rows(i), :] = _rmsnorm_rows(xo_ref[rows(i), :], g)
            return carry

        lax.fori_loop(0, n_rb, body, 0)


def _ffn_call(x, hg, gffn, wup, pf, wdn, gfin, *, seqs, tm, rb, cw=256):
    bsz, t_len, _ = x.shape
    n_f, _, fc2 = wup.shape
    fc = fc2 // 2
    nc = 512
    n_n = D_MODEL // nc
    n_t = t_len // tm
    n_s = n_f + n_n
    rb = min(rb, tm)

    def jf(s):
        return jnp.minimum(s, n_f - 1)

    def jn(s):
        return jnp.clip(s - n_f, 0, n_n - 1)

    if seqs:
        hg_spec = pl.BlockSpec((tm, fc), lambda b, t, s: (t, jf(s)))
        gout_shape = jax.ShapeDtypeStruct((t_len // SUBLANES, CONV_F - 1, D_FF), F32)
        gout_spec = pl.BlockSpec((tm // SUBLANES, CONV_F - 1, fc), lambda b, t, s: (t, 0, jf(s)))
        carry_scratch = []
    else:
        hg_spec = pl.BlockSpec((SUBLANES, fc), lambda b, t, s: (0, jf(s)))
        gout_shape = jax.ShapeDtypeStruct((bsz, n_t, CONV_F - 1, D_FF), F32)
        gout_spec = pl.BlockSpec((None, None, CONV_F - 1, fc), lambda b, t, s: (b, t, 0, jf(s)))
        carry_scratch = [pltpu.VMEM((n_f, SUBLANES, fc), F32)]

    in_specs = [
        pl.BlockSpec((None, tm, D_MODEL), lambda b, t, s: (b, t, 0)),
        hg_spec,
        pl.BlockSpec((1, D_MODEL), lambda b, t, s: (0, 0)),
        pl.BlockSpec((None, D_MODEL, 2 * fc), lambda b, t, s: (jf(s), 0, 0)),
        pl.BlockSpec((8, fc), lambda b, t, s: (0, jf(s))),
        pl.BlockSpec((D_FF, nc), lambda b, t, s: (0, jn(s))),
        pl.BlockSpec((1, D_MODEL), lambda b, t, s: (0, 0)),
    ]
    scratch = [
        pltpu.VMEM((tm, D_MODEL), BF16),
        pltpu.VMEM((tm, 2 * fc), F32),
        pltpu.VMEM((tm, D_FF), BF16),
        pltpu.VMEM((tm, D_MODEL), F32),
    ] + carry_scratch
    kern = functools.partial(_ffn_kernel, seqs=seqs, tm=tm, n_f=n_f, n_n=n_n, fc=fc, nc=nc,
                             rb=rb, cw=cw)
    return pl.pallas_call(
        kern,
        grid=(bsz, n_t, n_s),
        in_specs=in_specs,
        out_specs=[pl.BlockSpec((None, tm, D_MODEL), lambda b, t, s: (b, t, 0)), gout_spec],
        out_shape=[jax.ShapeDtypeStruct((bsz, t_len, D_MODEL), F32), gout_shape],
        scratch_shapes=scratch,
        compiler_params=pltpu.CompilerParams(
            dimension_semantics=("arbitrary", "arbitrary", "arbitrary"),
            vmem_limit_bytes=VMEM_LIMIT_BYTES),
        name="ffn_seqs" if seqs else "ffn_chain",
    )(x, hg, gffn, wup, pf, wdn, gfin)


def _pad_rows_front(a, rows):
    pad = [(0, 0)] * a.ndim
    pad[-2] = (rows - a.shape[-2], 0)
    return jnp.pad(a, pad)


def kernel(x_prompt, x_sample, state_lru_h, state_lru_conv, state_sconv, state_ffn_conv, meta_tokens, g_mix, w_in, conv_a_w, conv_a_b, w_gate_a, b_gate_a, w_gate_x, b_gate_x, lru_lambda, conv_b_w, g_out_a, g_out_b, w_o, g_ffn, w_up, conv_f_w, conv_f_b, w_down, g_final):
    depth = w_in.shape[0]
    assert depth == 1, "the kernels implement a single layer"
    ca, cb, fc = 512, 256, 512
    n_a, n_b, n_f = D_A // ca, D_B // cb, D_FF // fc

    wi = w_in[0].astype(BF16)
    xa_w, ga_w, gb_w, gc_w, vb_w = (wi[:, :D_A], wi[:, D_A:2 * D_A], wi[:, 2 * D_A:2 * D_A + D_B],
                                    wi[:, 2 * D_A + D_B:2 * D_A + 2 * D_B], wi[:, 2 * D_A + 2 * D_B:])
    wa = jnp.stack([jnp.concatenate([xa_w[:, j * ca:(j + 1) * ca], ga_w[:, j * ca:(j + 1) * ca]], axis=1)
                    for j in range(n_a)])
    wb = jnp.stack([jnp.concatenate([gb_w[:, j * cb:(j + 1) * cb], gc_w[:, j * cb:(j + 1) * cb],
                                     vb_w[:, j * cb:(j + 1) * cb]], axis=1) for j in range(n_b)])
    wg = jnp.concatenate([w_gate_a[0], w_gate_x[0]], axis=-1).astype(BF16)
    wo = w_o[0].astype(BF16)
    wu = w_up[0].astype(BF16)
    wup = jnp.stack([jnp.concatenate([wu[:, j * fc:(j + 1) * fc], wu[:, D_FF + j * fc:D_FF + (j + 1) * fc]], axis=1)
                     for j in range(n_f)])
    wdn = w_down[0].astype(BF16)
    pa = jnp.concatenate([conv_a_w[0], conv_a_b, b_gate_a, b_gate_x, lru_lambda, g_out_a,
                          jnp.zeros((16 - CONV_A - 5, D_A), F32)], axis=0)
    pb = jnp.concatenate([conv_b_w[0], g_out_b, jnp.zeros((8 - CONV_B - 1, D_B), F32)], axis=0)
    pf = jnp.concatenate([conv_f_w[0], conv_f_b, jnp.zeros((8 - CONV_F - 1, D_FF), F32)], axis=0)
    gmix, gffn, gfin = g_mix, g_ffn, g_final[None]

    def run(x, h0, hxa, hu, hg, *, seqs, tm):
        x1, h_new, xa_new, u_new = _mixer_call(x, h0, hxa, hu, gmix, wa, wg, pa, wb, pb, wo,
                                               seqs=seqs, tm=tm, rb=64)
        y, g_new = _ffn_call(x1, hg, gffn, wup, pf, wdn, gfin, seqs=seqs, tm=tm, rb=64)
        if not seqs:
            h_new, xa_new, u_new, g_new = (a[:, -1] for a in (h_new, xa_new, u_new, g_new))
        return y, h_new, xa_new, u_new, g_new

    n_meta = meta_tokens.shape[0]
    _, m_h, m_xa, m_u, m_g = run(
        meta_tokens[None], jnp.zeros((SUBLANES, D_A), F32), jnp.zeros((SUBLANES, D_A), F32),
        jnp.zeros((SUBLANES, D_B), F32), jnp.zeros((SUBLANES, D_FF), F32), seqs=False, tm=n_meta)

    y_p, p_h, p_xa, p_u, p_g = run(
        x_prompt, jnp.broadcast_to(m_h[0], (SUBLANES, D_A)), _pad_rows_front(m_xa[0], SUBLANES),
        _pad_rows_front(m_u[0], SUBLANES), _pad_rows_front(m_g[0], SUBLANES), seqs=False, tm=512)

    n_seq, t_s, _ = x_sample.shape
    assert t_s == SUBLANES
    y_s, s_h, s_xa, s_u, s_g = run(
        x_sample.reshape(1, n_seq * t_s, D_MODEL), state_lru_h[0],
        _pad_rows_front(state_lru_conv[0], SUBLANES).reshape(n_seq * t_s, D_A),
        _pad_rows_front(state_sconv[0], SUBLANES).reshape(n_seq * t_s, D_B),
        _pad_rows_front(state_ffn_conv[0], SUBLANES).reshape(n_seq * t_s, D_FF), seqs=True, tm=512)

    return (y_p, y_s.reshape(n_seq, t_s, D_MODEL),
            p_h.reshape(1, -1, D_A), p_xa[None], p_u[None], p_g[None],
            s_h[None], s_xa[None], s_u[None], s_g[None])
```
